```python
import jax, jax.numpy as jnp
from jax import lax
import numpy as np

D_MODEL = 2048
BATCH = 8
SEQ = 2048
DEPTH = 4
DEC_BATCH = 4
DEC_SEQ = 4096
PAST_LEN = 128

HEAD_DIM = 128
BLOCK = 128
WINDOW = 128
ATTN_WIDTH = D_MODEL * 3 // 4
N_HEADS = ATTN_WIDTH // HEAD_DIM
N_KV_HEADS = N_HEADS // 3
GROUP = N_HEADS // N_KV_HEADS
FOURIER_WIDTH = D_MODEL - ATTN_WIDTH
FOURIER_DIM = 128
N_FOURIER = FOURIER_WIDTH // FOURIER_DIM
CONV_WIDTH = D_MODEL // 2
CONV_K = 3
SG_WIDTH = D_MODEL - CONV_WIDTH
SG_DIM = 128
N_SG = SG_WIDTH // SG_DIM
CHUNK = 128
N_BUCKETS = 32
MAX_DISTANCE = 128
EPS = 1e-6
N_EVEN = (DEPTH + 1) // 2
N_ODD = DEPTH // 2
EVEN_SPLITS = [ATTN_WIDTH, N_KV_HEADS * HEAD_DIM, N_KV_HEADS * HEAD_DIM, FOURIER_WIDTH, ATTN_WIDTH, FOURIER_WIDTH]
ODD_SPLITS = [CONV_WIDTH, CONV_WIDTH, CONV_WIDTH, CONV_WIDTH, SG_WIDTH, SG_WIDTH, SG_WIDTH]
EVEN_IN = sum(EVEN_SPLITS)
ODD_IN = sum(ODD_SPLITS)

kernel_name = "hybrid_bidir_window_fnet_conv_gmlp"


def _rmsnorm(x, g):
    xf = x.astype(jnp.float32)
    r = lax.rsqrt(jnp.mean(xf * xf, axis=-1, keepdims=True) + EPS)
    return (xf * r).astype(x.dtype) * g


def _split(z, sizes):
    idx = [int(i) for i in np.cumsum(sizes)[:-1]]
    return jnp.split(z, idx, axis=-1)


def _t5_bucket(rel):
    nb = N_BUCKETS // 2
    ret = (rel > 0).astype(np.int32) * nb
    n = np.abs(rel)
    max_exact = nb // 2
    large = max_exact + (np.log(np.maximum(n, 1) / max_exact) / np.log(MAX_DISTANCE / max_exact)
                         * (nb - max_exact)).astype(np.int32)
    large = np.minimum(large, nb - 1)
    return (ret + np.where(n < max_exact, n, large)).astype(np.int32)


def _window_attention(q, k, v, rel_bias, sink):
    B, S = q.shape[0], q.shape[1]
    nb = S // BLOCK
    qb = q.reshape(B, nb, BLOCK, N_KV_HEADS, GROUP, HEAD_DIM)
    pad = ((0, 0), (BLOCK, BLOCK), (0, 0), (0, 0))

    def bands(t):
        tb = jnp.pad(t, pad).reshape(B, nb + 2, BLOCK, N_KV_HEADS, HEAD_DIM)
        return jnp.concatenate([tb[:, :-2], tb[:, 1:-1], tb[:, 2:]], axis=2)

    kb, vb = bands(k), bands(v)
    r = np.arange(BLOCK)[:, None]
    c = np.arange(3 * BLOCK)[None, :]
    rel = c - BLOCK - r
    pos_k = np.arange(nb)[:, None, None] * BLOCK + c[None] - BLOCK
    mask = jnp.asarray((np.abs(rel) <= WINDOW)[None] & (pos_k >= 0) & (pos_k < S))
    bias = jnp.take(rel_bias, jnp.asarray(_t5_bucket(rel)), axis=0)
    bias = jnp.transpose(bias, (2, 0, 1)).reshape(N_KV_HEADS, GROUP, BLOCK, 3 * BLOCK).astype(jnp.float32)

    s = jnp.einsum('bnqhgd,bnkhd->bnhgqk', qb, kb, preferred_element_type=jnp.float32) * (HEAD_DIM ** -0.5)
    s = jnp.where(mask[None, :, None, None], s + bias[None, None], -1e30)
    sink_l = sink.astype(jnp.float32).reshape(N_KV_HEADS, GROUP)[None, None, :, :, None, None]
    m = jnp.maximum(jnp.max(s, axis=-1, keepdims=True), sink_l)
    p = jnp.exp(s - m)
    denom = jnp.sum(p, axis=-1, keepdims=True) + jnp.exp(sink_l - m)
    o = jnp.einsum('bnhgqk,bnkhd->bnqhgd', (p / denom).astype(vb.dtype), vb)
    return o.reshape(B, S, N_HEADS * HEAD_DIM)


def _fourier(f, w_f, b_f):
    ff = jnp.fft.fft2(f.astype(jnp.float32), axes=(1, 3), norm='ortho').real.astype(f.dtype)
    return jnp.einsum('bsgc,gcd->bsgd', ff, w_f) + b_f


def _short_conv(h, w):
    hp = jnp.pad(h, ((0, 0), (1, 1), (0, 0)))
    return hp[:, :-2] * w[0] + hp[:, 1:-1] * w[1] + hp[:, 2:] * w[2]


def _spatial_gate(u, v, v_gain, w_s, b_s):
    B, S = u.shape[0], u.shape[1]
    nc = S // CHUNK
    vn = _rmsnorm(v.reshape(B, S, N_SG, SG_DIM), v_gain.reshape(N_SG, SG_DIM))
    vc = vn.reshape(B, nc, CHUNK, N_SG, SG_DIM)
    sv = jnp.einsum('gpq,bnqgc->bnpgc', w_s, vc) + jnp.transpose(b_s)[None, None, :, :, None]
    return u * sv.reshape(B, S, SG_WIDTH)


def _even_layer(x, g, w_in, w_out, q_gain, k_gain, sink, w_f, b_f, rel_bias):
    B, S = x.shape[0], x.shape[1]
    z = _rmsnorm(x, g) @ w_in
    q, k, v, f, ga, gf = _split(z, EVEN_SPLITS)
    q = _rmsnorm(q.reshape(B, S, N_HEADS, HEAD_DIM), q_gain)
    k = _rmsnorm(k.reshape(B, S, N_KV_HEADS, HEAD_DIM), k_gain)
    v = v.reshape(B, S, N_KV_HEADS, HEAD_DIM)
    a = _window_attention(q, k, v, rel_bias, sink) * jax.nn.silu(ga)
    fo = _fourier(f.reshape(B, S, N_FOURIER, FOURIER_DIM), w_f, b_f).reshape(B, S, FOURIER_WIDTH) * jax.nn.silu(gf)
    return x + jnp.concatenate([a, fo], axis=-1) @ w_out


def _odd_layer(x, g, w_in, conv_w, v_gain, w_s, b_s, w_out):
    z = _rmsnorm(x, g) @ w_in
    h, bg, cg, gc, u, v, gd = _split(z, ODD_SPLITS)
    co = (bg * _short_conv(cg * h, conv_w)) * jax.nn.silu(gc)
    so = _spatial_gate(u, v, v_gain, w_s, b_s) * jax.nn.silu(gd)
    return x + jnp.concatenate([co, so], axis=-1) @ w_out


def _trunk(x, norm_gain, rel_bias, w_in_e, w_out_e, q_gain, k_gain, sink, w_f, b_f,
           w_in_o, conv_w, v_gain, w_s, b_s, w_out_o):
    for l in range(DEPTH):
        i = l // 2
        if l % 2 == 0:
            x = _even_layer(x, norm_gain[l], w_in_e[i], w_out_e[i], q_gain[i], k_gain[i], sink[i],
                            w_f[i], b_f[i], rel_bias)
        else:
            x = _odd_layer(x, norm_gain[l], w_in_o[i], conv_w[i], v_gain[i], w_s[i], b_s[i], w_out_o[i])
    return x


def setup_inputs(seed: int = 0) -> dict:
    key = jax.random.key(seed)
    ks = jax.random.split(key, 20)
    nrm = lambda k, shape, s: jax.random.normal(k, shape, jnp.float32) * s
    return {
        "x_prompt": nrm(ks[0], (BATCH, SEQ, D_MODEL), 1.0),
        "x_sample": nrm(ks[1], (DEC_BATCH, DEC_SEQ, D_MODEL), 1.0),
        "norm_gain": 1.0 + nrm(ks[2], (DEPTH, D_MODEL), 0.02),
        "rel_bias": nrm(ks[3], (N_BUCKETS, N_HEADS), 0.1),
        "w_in_e": nrm(ks[4], (N_EVEN, D_MODEL, EVEN_IN), D_MODEL ** -0.5),
        "w_out_e": nrm(ks[5], (N_EVEN, ATTN_WIDTH + FOURIER_WIDTH, D_MODEL), (ATTN_WIDTH + FOURIER_WIDTH) ** -0.5),
        "q_gain": 1.0 + nrm(ks[6], (N_EVEN, HEAD_DIM), 0.02),
        "k_gain": 1.0 + nrm(ks[7], (N_EVEN, HEAD_DIM), 0.02),
        "sink": nrm(ks[8], (N_EVEN, N_HEADS), 0.1),
        "w_f": nrm(ks[9], (N_EVEN, N_FOURIER, FOURIER_DIM, FOURIER_DIM), FOURIER_DIM ** -0.5),
        "b_f": nrm(ks[10], (N_EVEN, N_FOURIER, FOURIER_DIM), 0.02),
        "w_in_o": nrm(ks[11], (N_ODD, D_MODEL, ODD_IN), D_MODEL ** -0.5),
        "conv_w": nrm(ks[12], (N_ODD, CONV_K, CONV_WIDTH), CONV_K ** -0.5),
        "v_gain": 1.0 + nrm(ks[13], (N_ODD, SG_WIDTH), 0.02),
        "w_s": nrm(ks[14], (N_ODD, N_SG, CHUNK, CHUNK), CHUNK ** -0.5),
        "b_s": nrm(ks[15], (N_ODD, N_SG, CHUNK), 0.02),
        "w_out_o": nrm(ks[16], (N_ODD, CONV_WIDTH + SG_WIDTH, D_MODEL), (CONV_WIDTH + SG_WIDTH) ** -0.5),
    }


def reference(x_prompt, x_sample, norm_gain, rel_bias, w_in_e, w_out_e, q_gain, k_gain, sink, w_f, b_f,
              w_in_o, conv_w, v_gain, w_s, b_s, w_out_o):
    y_prompt = _trunk(x_prompt, norm_gain, rel_bias, w_in_e, w_out_e, q_gain, k_gain, sink, w_f, b_f,
                      w_in_o, conv_w, v_gain, w_s, b_s, w_out_o)
    y_sample = _trunk(x_sample, norm_gain, rel_bias, w_in_e, w_out_e, q_gain, k_gain, sink, w_f, b_f,
                      w_in_o, conv_w, v_gain, w_s, b_s, w_out_o)
    return (y_prompt, y_sample)
```

```python
import functools
import math

import numpy as np
import jax
import jax.numpy as jnp
from jax import lax
from jax.experimental import pallas as pl
from jax.experimental.pallas import tpu as pltpu

F32 = jnp.float32
BF16 = jnp.bfloat16

D_MODEL = 2048
HEAD_DIM = 128
BLOCK = 128
ATTN_WIDTH = 1536
N_HEADS = 12
N_KV_HEADS = 4
GROUP = 3
KV_WIDTH = N_KV_HEADS * HEAD_DIM
FOURIER_WIDTH = 512
FOURIER_DIM = 128
N_FOURIER = 4
CONV_WIDTH = 1024
SG_WIDTH = 1024
SG_DIM = 128
N_SG = 8
N_BUCKETS = 32
MAX_DISTANCE = 128
EPS = 1e-6
EVEN_IN = 5120
ODD_IN = 7168
NEG = -1e30

E_Q, E_K, E_V, E_F, E_GA, E_GF = 0, 1536, 2048, 2560, 3072, 4608
O_H, O_BG, O_CG, O_GC, O_U, O_V, O_GD = 0, 1024, 2048, 3072, 4096, 5120, 6144

V7X_VMEM_LIMIT_BYTES = 56 * 1024 * 1024
IN_TM, IN_TN = 1024, 1024
OUT_TM = 256
DFT_TM = 512


def _const_spec(shape):
    nd = len(shape)
    return pl.BlockSpec(shape, lambda *_: (0,) * nd, pipeline_mode=pl.Buffered(1))


def _silu(v):
    return v * (1.0 / (1.0 + jnp.exp(-v)))


def _in_proj_kernel(x_ref, g_ref, w_ref, gain_ref, o_ref, xn_ref, *, norm_lo, norm_hi, silu_tiles):
    j = pl.program_id(1)

    @pl.when(j == 0)
    def _():
        xf = x_ref[...]
        r = lax.rsqrt(jnp.mean(xf * xf, axis=-1, keepdims=True) + EPS)
        xn_ref[...] = (xf * r * g_ref[...]).astype(BF16)

    z = jnp.dot(xn_ref[...], w_ref[...], preferred_element_type=F32)
    tn = z.shape[1]
    is_norm = jnp.logical_and(j >= norm_lo, j < norm_hi)
    is_silu = functools.reduce(jnp.logical_or, [j == t for t in silu_tiles])

    @pl.when(is_norm)
    def _():
        for c in range(0, tn, HEAD_DIM):
            zh = z[:, c:c + HEAD_DIM]
            r = lax.rsqrt(jnp.mean(zh * zh, axis=-1, keepdims=True) + EPS)
            o_ref[:, c:c + HEAD_DIM] = (zh * r * gain_ref[:, c:c + HEAD_DIM]).astype(BF16)

    @pl.when(is_silu)
    def _():
        o_ref[...] = _silu(z).astype(BF16)

    @pl.when(jnp.logical_not(jnp.logical_or(is_norm, is_silu)))
    def _():
        o_ref[...] = z.astype(BF16)


def _in_proj(x2d, g, w, gain_cols, norm_lo, silu_tiles):
    t, d = x2d.shape
    n = w.shape[1]
    tm, tn = min(IN_TM, t), IN_TN
    n_norm = gain_cols.shape[1] // tn
    kern = functools.partial(_in_proj_kernel, norm_lo=norm_lo, norm_hi=norm_lo + n_norm, silu_tiles=silu_tiles)
    return pl.pallas_call(
        kern,
        out_shape=jax.ShapeDtypeStruct((t, n), BF16),
        grid=(t // tm, n // tn),
        in_specs=[
            pl.BlockSpec((tm, d), lambda i, j: (i, 0)),
            pl.BlockSpec((1, d), lambda i, j: (0, 0)),
            pl.BlockSpec((d, tn), lambda i, j: (0, j)),
            pl.BlockSpec((1, tn), lambda i, j: (0, jnp.clip(j - norm_lo, 0, n_norm - 1))),
        ],
        out_specs=pl.BlockSpec((tm, tn), lambda i, j: (i, j)),
        scratch_shapes=[pltpu.VMEM((tm, d), BF16)],
        compiler_params=pltpu.CompilerParams(
            dimension_semantics=("arbitrary", "arbitrary"), vmem_limit_bytes=V7X_VMEM_LIMIT_BYTES),
        name="in_proj",
    )(x2d, g, w, gain_cols)


def _fourier_kernel(c_ref, s_ref, f_ref, gate_ref, cc_ref, sc_ref, wf_ref, bf_ref, o_ref):
    f = f_ref[...]
    yc = jnp.dot(c_ref[...], f, preferred_element_type=F32)
    ys = jnp.dot(s_ref[...], f, preferred_element_type=F32)
    for g in range(N_FOURIER):
        sl = slice(g * FOURIER_DIM, (g + 1) * FOURIER_DIM)
        ff = (jnp.dot(yc[:, sl].astype(BF16), cc_ref[...], preferred_element_type=F32)
              - jnp.dot(ys[:, sl].astype(BF16), sc_ref[...], preferred_element_type=F32))
        out = jnp.dot(ff.astype(BF16), wf_ref[g], preferred_element_type=F32) + bf_ref[:, sl]
        o_ref[:, sl] = (out * gate_ref[:, sl].astype(F32)).astype(BF16)


def _fourier(z, cos_s, sin_s, cos_c, sin_c, w_f, b_f, batch, seq):
    t = z.shape[0]
    tm = min(DFT_TM, seq)
    nt = seq // tm
    return pl.pallas_call(
        _fourier_kernel,
        out_shape=jax.ShapeDtypeStruct((t, FOURIER_WIDTH), BF16),
        grid=(nt, batch),
        in_specs=[
            pl.BlockSpec((tm, seq), lambda i, b: (i, 0)),
            pl.BlockSpec((tm, seq), lambda i, b: (i, 0)),
            pl.BlockSpec((seq, FOURIER_WIDTH), lambda i, b: (b, E_F // FOURIER_WIDTH)),
            pl.BlockSpec((tm, FOURIER_WIDTH), lambda i, b: (b * nt + i, E_GF // FOURIER_WIDTH)),
            _const_spec((FOURIER_DIM, FOURIER_DIM)),
            _const_spec((FOURIER_DIM, FOURIER_DIM)),
            _const_spec((N_FOURIER, FOURIER_DIM, FOURIER_DIM)),
            _const_spec((1, FOURIER_WIDTH)),
        ],
        out_specs=pl.BlockSpec((tm, FOURIER_WIDTH), lambda i, b: (b * nt + i, 0)),
        compiler_params=pltpu.CompilerParams(
            dimension_semantics=("arbitrary", "arbitrary"), vmem_limit_bytes=V7X_VMEM_LIMIT_BYTES),
        name="fourier",
    )(cos_s, sin_s, z, z, cos_c, sin_c, w_f, b_f)


def _even_out_kernel(q_ref, k_ref, kp_ref, kn_ref, v_ref, vp_ref, vn_ref, ga_ref, fo_ref, x_ref,
                     bias_ref, sink_ref, wout_ref, y_ref, act_ref, *, nblk, nt):
    i = pl.program_id(1)
    col = lax.broadcasted_iota(jnp.int32, (1, 3 * BLOCK), 1)
    first_edge = jnp.where(jnp.logical_and(i == 0, col < BLOCK), NEG, 0.0).astype(F32)
    last_edge = jnp.where(jnp.logical_and(i == nt - 1, col >= 2 * BLOCK), NEG, 0.0).astype(F32)

    for qb in range(nblk):
        rows = slice(qb * BLOCK, (qb + 1) * BLOCK)
        prev_rows = slice((qb - 1) * BLOCK, qb * BLOCK)
        next_rows = slice((qb + 1) * BLOCK, (qb + 2) * BLOCK)
        for h in range(N_KV_HEADS):
            hc = slice(h * HEAD_DIM, (h + 1) * HEAD_DIM)
            kband = jnp.concatenate([
                kp_ref[:, hc] if qb == 0 else k_ref[prev_rows, hc],
                k_ref[rows, hc],
                kn_ref[:, hc] if qb == nblk - 1 else k_ref[next_rows, hc]], axis=0)
            vband = jnp.concatenate([
                vp_ref[:, hc] if qb == 0 else v_ref[prev_rows, hc],
                v_ref[rows, hc],
                vn_ref[:, hc] if qb == nblk - 1 else v_ref[next_rows, hc]], axis=0)
            qs = jnp.concatenate(
                [q_ref[rows, (GROUP * h + g) * HEAD_DIM:(GROUP * h + g + 1) * HEAD_DIM] for g in range(GROUP)],
                axis=0)
            s = lax.dot_general(qs, kband, (((1,), (1,)), ((), ())), preferred_element_type=F32)
            s = s + bias_ref[h]
            if qb == 0:
                s = s + first_edge
            if qb == nblk - 1:
                s = s + last_edge
            sink = sink_ref[h]
            m = jnp.maximum(jnp.max(s, axis=-1, keepdims=True), sink)
            p = jnp.exp(s - m)
            denom = jnp.sum(p, axis=-1, keepdims=True) + jnp.exp(sink - m)
            o = jnp.dot(p.astype(BF16), vband, preferred_element_type=F32) / denom
            for g in range(GROUP):
                cs = slice((GROUP * h + g) * HEAD_DIM, (GROUP * h + g + 1) * HEAD_DIM)
                act_ref[rows, cs] = (o[g * BLOCK:(g + 1) * BLOCK] * ga_ref[rows, cs].astype(F32)).astype(BF16)
    act_ref[:, ATTN_WIDTH:] = fo_ref[...]
    y_ref[...] = x_ref[...] + jnp.dot(act_ref[...], wout_ref[...], preferred_element_type=F32)


def _even_out(z, fo, x2d, bias_tbl, sink_col, w_out, batch, seq):
    t = x2d.shape[0]
    tm = min(OUT_TM, seq)
    nblk = tm // BLOCK
    nt = seq // tm
    n_blocks = t // BLOCK
    kern = functools.partial(_even_out_kernel, nblk=nblk, nt=nt)

    def row(b, i):
        return b * nt + i

    def prev_blk(b, i):
        return jnp.maximum(row(b, i) * nblk - 1, 0)

    def next_blk(b, i):
        return jnp.minimum((row(b, i) + 1) * nblk, n_blocks - 1)

    kcol, vcol = E_K // KV_WIDTH, E_V // KV_WIDTH
    return pl.pallas_call(
        kern,
        out_shape=jax.ShapeDtypeStruct((t, D_MODEL), F32),
        grid=(batch, nt),
        in_specs=[
            pl.BlockSpec((tm, ATTN_WIDTH), lambda b, i: (row(b, i), E_Q // ATTN_WIDTH)),
            pl.BlockSpec((tm, KV_WIDTH), lambda b, i: (row(b, i), kcol)),
            pl.BlockSpec((BLOCK, KV_WIDTH), lambda b, i: (prev_blk(b, i), kcol)),
            pl.BlockSpec((BLOCK, KV_WIDTH), lambda b, i: (next_blk(b, i), kcol)),
            pl.BlockSpec((tm, KV_WIDTH), lambda b, i: (row(b, i), vcol)),
            pl.BlockSpec((BLOCK, KV_WIDTH), lambda b, i: (prev_blk(b, i), vcol)),
            pl.BlockSpec((BLOCK, KV_WIDTH), lambda b, i: (next_blk(b, i), vcol)),
            pl.BlockSpec((tm, ATTN_WIDTH), lambda b, i: (row(b, i), E_GA // ATTN_WIDTH)),
            pl.BlockSpec((tm, FOURIER_WIDTH), lambda b, i: (row(b, i), 0)),
            pl.BlockSpec((tm, D_MODEL), lambda b, i: (row(b, i), 0)),
            _const_spec(bias_tbl.shape),
            _const_spec(sink_col.shape),
            _const_spec(w_out.shape),
        ],
        out_specs=pl.BlockSpec((tm, D_MODEL), lambda b, i: (row(b, i), 0)),
        scratch_shapes=[pltpu.VMEM((tm, D_MODEL), BF16)],
        compiler_params=pltpu.CompilerParams(
            dimension_semantics=("arbitrary", "arbitrary"), vmem_limit_bytes=V7X_VMEM_LIMIT_BYTES),
        name="even_out",
    )(z, z, z, z, z, z, z, z, fo, x2d, bias_tbl, sink_col, w_out)


HALO = 16


def _odd_out_kernel(z_ref, hp_ref, hn_ref, cp_ref, cn_ref, x_ref, convw_ref, ws_ref, bs_ref, wout_ref,
                    y_ref, act_ref, *, nchunk, nt):
    i = pl.program_id(1)
    tm = nchunk * BLOCK
    row = lax.broadcasted_iota(jnp.int32, (tm, 1), 0)
    has_prev = (i > 0).astype(F32)
    has_next = (i < nt - 1).astype(F32)
    for c0 in range(0, CONV_WIDTH, HEAD_DIM):
        cs = slice(c0, c0 + HEAD_DIM)
        c = z_ref[:, O_CG + c0:O_CG + c0 + HEAD_DIM].astype(F32) * z_ref[:, O_H + c0:O_H + c0 + HEAD_DIM].astype(F32)
        c_before = cp_ref[HALO - 1:HALO, cs].astype(F32) * hp_ref[HALO - 1:HALO, cs].astype(F32) * has_prev
        c_after = cn_ref[0:1, cs].astype(F32) * hn_ref[0:1, cs].astype(F32) * has_next
        c_m1 = jnp.where(row == 0, c_before, pltpu.roll(c, 1, 0))
        c_p1 = jnp.where(row == tm - 1, c_after, pltpu.roll(c, tm - 1, 0))
        conv = c_m1 * convw_ref[0:1, cs] + c * convw_ref[1:2, cs] + c_p1 * convw_ref[2:3, cs]
        co = (z_ref[:, O_BG + c0:O_BG + c0 + HEAD_DIM].astype(F32) * conv) * z_ref[:, O_GC + c0:O_GC + c0 + HEAD_DIM].astype(F32)
        act_ref[:, cs] = co.astype(BF16)
    for ck in range(nchunk):
        rows = slice(ck * BLOCK, (ck + 1) * BLOCK)
        for g in range(N_SG):
            vn = z_ref[rows, O_V + g * SG_DIM:O_V + (g + 1) * SG_DIM]
            sv = jnp.dot(ws_ref[g], vn, preferred_element_type=F32) + bs_ref[g]
            u = z_ref[rows, O_U + g * SG_DIM:O_U + (g + 1) * SG_DIM].astype(F32)
            gd = z_ref[rows, O_GD + g * SG_DIM:O_GD + (g + 1) * SG_DIM].astype(F32)
            act_ref[rows, CONV_WIDTH + g * SG_DIM:CONV_WIDTH + (g + 1) * SG_DIM] = ((u * sv) * gd).astype(BF16)
    y_ref[...] = x_ref[...] + jnp.dot(act_ref[...], wout_ref[...], preferred_element_type=F32)


def _odd_out(z, x2d, conv_w, w_s, b_s, w_out, batch, seq):
    t = x2d.shape[0]
    tm = min(OUT_TM, seq)
    nchunk = tm // BLOCK
    nt = seq // tm
    per = tm // HALO
    n_halo = t // HALO
    kern = functools.partial(_odd_out_kernel, nchunk=nchunk, nt=nt)

    def row(b, i):
        return b * nt + i

    def prev_blk(b, i):
        return jnp.maximum(row(b, i) * per - 1, 0)

    def next_blk(b, i):
        return jnp.minimum((row(b, i) + 1) * per, n_halo - 1)

    hcol, ccol = O_H // CONV_WIDTH, O_CG // CONV_WIDTH
    return pl.pallas_call(
        kern,
        out_shape=jax.ShapeDtypeStruct((t, D_MODEL), F32),
        grid=(batch, nt),
        in_specs=[
            pl.BlockSpec((tm, ODD_IN), lambda b, i: (row(b, i), 0)),
            pl.BlockSpec((HALO, CONV_WIDTH), lambda b, i: (prev_blk(b, i), hcol)),
            pl.BlockSpec((HALO, CONV_WIDTH), lambda b, i: (next_blk(b, i), hcol)),
            pl.BlockSpec((HALO, CONV_WIDTH), lambda b, i: (prev_blk(b, i), ccol)),
            pl.BlockSpec((HALO, CONV_WIDTH), lambda b, i: (next_blk(b, i), ccol)),
            pl.BlockSpec((tm, D_MODEL), lambda b, i: (row(b, i), 0)),
            _const_spec(conv_w.shape),
            _const_spec(w_s.shape),
            _const_spec(b_s.shape),
            _const_spec(w_out.shape),
        ],
        out_specs=pl.BlockSpec((tm, D_MODEL), lambda b, i: (row(b, i), 0)),
        scratch_shapes=[pltpu.VMEM((tm, D_MODEL), BF16)],
        compiler_params=pltpu.CompilerParams(
            dimension_semantics=("arbitrary", "arbitrary"), vmem_limit_bytes=V7X_VMEM_LIMIT_BYTES),
        name="odd_out",
    )(z, z, z, z, z, x2d, conv_w, w_s, b_s, w_out)


def _t5_bucket(rel):
    nb = N_BUCKETS // 2
    ret = (rel > 0).astype(np.int32) * nb
    n = np.abs(rel)
    max_exact = nb // 2
    large = max_exact + (np.log(np.maximum(n, 1) / max_exact) / np.log(MAX_DISTANCE / max_exact)
                         * (nb - max_exact)).astype(np.int32)
    large = np.minimum(large, nb - 1)
    return (ret + np.where(n < max_exact, n, large)).astype(np.int32)


def _attention_bias_table(rel_bias):
    r = np.arange(BLOCK)[:, None]
    c = np.arange(3 * BLOCK)[None, :]
    rel = c - BLOCK - r
    bias = jnp.take(rel_bias.astype(F32), jnp.asarray(_t5_bucket(rel)), axis=0)
    bias = jnp.where(jnp.asarray(np.abs(rel) <= BLOCK)[:, :, None], bias, NEG)
    bias = jnp.transpose(bias, (2, 0, 1))
    return bias.reshape(N_KV_HEADS, GROUP * BLOCK, 3 * BLOCK)


def _dft_tables(n, scale):
    fine = 1
    while fine * fine < n:
        fine *= 2
    coarse = n // fine
    k = jnp.arange(n, dtype=jnp.int32)[None, :]
    ja = (jnp.arange(coarse, dtype=jnp.int32) * fine)[:, None]
    jb = jnp.arange(fine, dtype=jnp.int32)[:, None]
    w = 2.0 * math.pi / n
    ang_a = ((ja * k) % n).astype(F32) * w
    ang_b = ((jb * k) % n).astype(F32) * w
    ca, sa = (jnp.cos(ang_a) * scale)[:, None, :], (jnp.sin(ang_a) * scale)[:, None, :]
    cb, sb = jnp.cos(ang_b)[None, :, :], jnp.sin(ang_b)[None, :, :]
    cos = (ca * cb - sa * sb).reshape(n, n)
    sin = (sa * cb + ca * sb).reshape(n, n)
    return cos.astype(BF16), sin.astype(BF16)


def _trunk(x, p, dft_s, batch, seq):
    x2d = x.reshape(batch * seq, D_MODEL)
    for l in range(4):
        i = l // 2
        g = p["norm_gain"][l].reshape(1, D_MODEL)
        if l % 2 == 0:
            z = _in_proj(x2d, g, p["w_in_e"][i], p["qk_gain"][i], 0, (3, 4))
            fo = _fourier(z, dft_s[0], dft_s[1], p["dft_c"][0], p["dft_c"][1], p["w_f"][i], p["b_f"][i], batch, seq)
            x2d = _even_out(z, fo, x2d, p["bias_tbl"], p["sink_col"][i], p["w_out_e"][i], batch, seq)
        else:
            z = _in_proj(x2d, g, p["w_in_o"][i], p["v_gain"][i], O_V // IN_TN, (O_GC // IN_TN, O_GD // IN_TN))
            x2d = _odd_out(z, x2d, p["conv_w"][i], p["w_s"][i], p["b_s"][i], p["w_out_o"][i], batch, seq)
    return x2d.reshape(batch, seq, D_MODEL)


def _prepare_params(norm_gain, rel_bias, w_in_e, w_out_e, q_gain, k_gain, sink, w_f, b_f,
                    w_in_o, conv_w, v_gain, w_s, b_s, w_out_o):
    n_even, n_odd = w_in_e.shape[0], w_in_o.shape[0]
    scale = HEAD_DIM ** -0.5
    qk_gain = jnp.concatenate(
        [jnp.tile(q_gain.astype(F32) * scale, (1, N_HEADS)), jnp.tile(k_gain.astype(F32), (1, N_KV_HEADS))], axis=1)
    return {
        "norm_gain": norm_gain.astype(F32),
        "w_in_e": w_in_e.astype(BF16), "w_out_e": w_out_e.astype(BF16),
        "w_in_o": w_in_o.astype(BF16), "w_out_o": w_out_o.astype(BF16),
        "qk_gain": qk_gain.reshape(n_even, 1, ATTN_WIDTH + KV_WIDTH),
        "v_gain": v_gain.astype(F32).reshape(n_odd, 1, SG_WIDTH),
        "bias_tbl": _attention_bias_table(rel_bias),
        "sink_col": jnp.repeat(sink.astype(F32).reshape(n_even, N_KV_HEADS, GROUP), BLOCK, axis=2)[..., None],
        "w_f": w_f.astype(BF16),
        "b_f": b_f.astype(F32).reshape(n_even, 1, FOURIER_WIDTH),
        "dft_c": _dft_tables(FOURIER_DIM, FOURIER_DIM ** -0.5),
        "conv_w": conv_w.astype(F32),
        "w_s": w_s.astype(BF16),
        "b_s": jnp.broadcast_to(b_s.astype(F32)[..., None], b_s.shape + (SG_DIM,)),
    }


def _run_trunk(x, p):
    batch, seq = x.shape[0], x.shape[1]
    return _trunk(x, p, _dft_tables(seq, seq ** -0.5), batch, seq)


def kernel(x_prompt, x_sample, norm_gain, rel_bias, w_in_e, w_out_e, q_gain, k_gain, sink, w_f, b_f,
           w_in_o, conv_w, v_gain, w_s, b_s, w_out_o):
    p = _prepare_params(norm_gain, rel_bias, w_in_e, w_out_e, q_gain, k_gain, sink, w_f, b_f,
                        w_in_o, conv_w, v_gain, w_s, b_s, w_out_o)
    return (_run_trunk(x_prompt, p), _run_trunk(x_sample, p))
```

```python
import functools
import math

import numpy as np
import jax
import jax.numpy as jnp
from jax import lax
from jax.experimental import pallas as pl
from jax.experimental.pallas import tpu as pltpu

F32 = jnp.float32
BF16 = jnp.bfloat16

D_MODEL = 2048
HEAD_DIM = 128
BLOCK = 128
ATTN_WIDTH = 1536
N_HEADS = 12
N_KV_HEADS = 4
GROUP = 3
KV_WIDTH = N_KV_HEADS * HEAD_DIM
FOURIER_WIDTH = 512
FOURIER_DIM = 128
N_FOURIER = 4
CONV_WIDTH = 1024
SG_WIDTH = 1024
SG_DIM = 128
N_SG = 8
N_BUCKETS = 32
MAX_DISTANCE = 128
EPS = 1e-6
EVEN_IN = 5120
ODD_IN = 7168
NEG = -1e30

E_Q, E_K, E_V, E_F, E_GA, E_GF = 0, 1536, 2048, 2560, 3072, 4608
O_H, O_BG, O_CG, O_GC, O_U, O_V, O_GD = 0, 1024, 2048, 3072, 4096, 5120, 6144

V7X_VMEM_LIMIT_BYTES = 56 * 1024 * 1024
IN_TM, IN_TN = 1024, 1024
IN_CHUNK = 256
OUT_TM = 512
DFT_TM = 512


def _const_spec(shape):
    nd = len(shape)
    return pl.BlockSpec(shape, lambda *_: (0,) * nd, pipeline_mode=pl.Buffered(1))


def _silu(v):
    return v * (1.0 / (1.0 + jnp.exp(-v)))


def _in_proj_kernel(x_ref, g_ref, w_ref, gain_ref, o_ref, xn_ref, *, norm_lo, norm_hi, silu_tiles):
    j = pl.program_id(1)

    @pl.when(j == 0)
    def _():
        xf = x_ref[...]
        r = lax.rsqrt(jnp.mean(xf * xf, axis=-1, keepdims=True) + EPS)
        xn_ref[...] = (xf * r * g_ref[...]).astype(BF16)

    tn = w_ref.shape[1]
    is_norm = jnp.logical_and(j >= norm_lo, j < norm_hi)
    is_silu = functools.reduce(jnp.logical_or, [j == t for t in silu_tiles])

    def chunks():
        for c in range(0, tn, IN_CHUNK):
            yield c, jnp.dot(xn_ref[...], w_ref[:, c:c + IN_CHUNK], preferred_element_type=F32)

    @pl.when(is_norm)
    def _():
        for c0, z in chunks():
            for c in range(0, IN_CHUNK, HEAD_DIM):
                zh = z[:, c:c + HEAD_DIM]
                r = lax.rsqrt(jnp.mean(zh * zh, axis=-1, keepdims=True) + EPS)
                o_ref[:, c0 + c:c0 + c + HEAD_DIM] = (zh * r * gain_ref[:, c0 + c:c0 + c + HEAD_DIM]).astype(BF16)

    @pl.when(is_silu)
    def _():
        for c0, z in chunks():
            o_ref[:, c0:c0 + IN_CHUNK] = _silu(z).astype(BF16)

    @pl.when(jnp.logical_not(jnp.logical_or(is_norm, is_silu)))
    def _():
        for c0, z in chunks():
            o_ref[:, c0:c0 + IN_CHUNK] = z.astype(BF16)


def _in_proj(x2d, g, w, gain_cols, norm_lo, silu_tiles):
    t, d = x2d.shape
    n = w.shape[1]
    tm, tn = min(IN_TM, t), IN_TN
    n_norm = gain_cols.shape[1] // tn
    kern = functools.partial(_in_proj_kernel, norm_lo=norm_lo, norm_hi=norm_lo + n_norm, silu_tiles=silu_tiles)
    return pl.pallas_call(
        kern,
        out_shape=jax.ShapeDtypeStruct((t, n), BF16),
        grid=(t // tm, n // tn),
        in_specs=[
            pl.BlockSpec((tm, d), lambda i, j: (i, 0)),
            pl.BlockSpec((1, d), lambda i, j: (0, 0)),
            pl.BlockSpec((d, tn), lambda i, j: (0, j)),
            pl.BlockSpec((1, tn), lambda i, j: (0, jnp.clip(j - norm_lo, 0, n_norm - 1))),
        ],
        out_specs=pl.BlockSpec((tm, tn), lambda i, j: (i, j)),
        scratch_shapes=[pltpu.VMEM((tm, d), BF16)],
        compiler_params=pltpu.CompilerParams(
            dimension_semantics=("arbitrary", "arbitrary"), vmem_limit_bytes=V7X_VMEM_LIMIT_BYTES),
        name="in_proj",
    )(x2d, g, w, gain_cols)


def _fourier_kernel(c_ref, s_ref, f_ref, gate_ref, cc_ref, sc_ref, wf_ref, bf_ref, o_ref):
    f = f_ref[...]
    yc = jnp.dot(c_ref[...], f, preferred_element_type=F32)
    ys = jnp.dot(s_ref[...], f, preferred_element_type=F32)
    for g in range(N_FOURIER):
        sl = slice(g * FOURIER_DIM, (g + 1) * FOURIER_DIM)
        ff = (jnp.dot(yc[:, sl].astype(BF16), cc_ref[...], preferred_element_type=F32)
              - jnp.dot(ys[:, sl].astype(BF16), sc_ref[...], preferred_element_type=F32))
        out = jnp.dot(ff.astype(BF16), wf_ref[g], preferred_element_type=F32) + bf_ref[:, sl]
        o_ref[:, sl] = (out * gate_ref[:, sl].astype(F32)).astype(BF16)


def _fourier(z, cos_s, sin_s, cos_c, sin_c, w_f, b_f, batch, seq):
    t = z.shape[0]
    tm = min(DFT_TM, seq)
    nt = seq // tm
    return pl.pallas_call(
        _fourier_kernel,
        out_shape=jax.ShapeDtypeStruct((t, FOURIER_WIDTH), BF16),
        grid=(nt, batch),
        in_specs=[
            pl.BlockSpec((tm, seq), lambda i, b: (i, 0)),
            pl.BlockSpec((tm, seq), lambda i, b: (i, 0)),
            pl.BlockSpec((seq, FOURIER_WIDTH), lambda i, b: (b, E_F // FOURIER_WIDTH)),
            pl.BlockSpec((tm, FOURIER_WIDTH), lambda i, b: (b * nt + i, E_GF // FOURIER_WIDTH)),
            _const_spec((FOURIER_DIM, FOURIER_DIM)),
            _const_spec((FOURIER_DIM, FOURIER_DIM)),
            _const_spec((N_FOURIER, FOURIER_DIM, FOURIER_DIM)),
            _const_spec((1, FOURIER_WIDTH)),
        ],
        out_specs=pl.BlockSpec((tm, FOURIER_WIDTH), lambda i, b: (b * nt + i, 0)),
        compiler_params=pltpu.CompilerParams(
            dimension_semantics=("arbitrary", "arbitrary"), vmem_limit_bytes=V7X_VMEM_LIMIT_BYTES),
        name="fourier",
    )(cos_s, sin_s, z, z, cos_c, sin_c, w_f, b_f)


def _even_out_kernel(q_ref, k_ref, kp_ref, kn_ref, v_ref, vp_ref, vn_ref, ga_ref, fo_ref, x_ref,
                     bias_ref, sink_ref, wout_ref, y_ref, act_ref, *, nblk, nt):
    i = pl.program_id(1)
    col = lax.broadcasted_iota(jnp.int32, (1, 3 * BLOCK), 1)
    first_edge = jnp.where(jnp.logical_and(i == 0, col < BLOCK), NEG, 0.0).astype(F32)
    last_edge = jnp.where(jnp.logical_and(i == nt - 1, col >= 2 * BLOCK), NEG, 0.0).astype(F32)

    for qb in range(nblk):
        rows = slice(qb * BLOCK, (qb + 1) * BLOCK)
        prev_rows = slice((qb - 1) * BLOCK, qb * BLOCK)
        next_rows = slice((qb + 1) * BLOCK, (qb + 2) * BLOCK)
        for h in range(N_KV_HEADS):
            hc = slice(h * HEAD_DIM, (h + 1) * HEAD_DIM)
            kband = jnp.concatenate([
                kp_ref[:, hc] if qb == 0 else k_ref[prev_rows, hc],
                k_ref[rows, hc],
                kn_ref[:, hc] if qb == nblk - 1 else k_ref[next_rows, hc]], axis=0)
            vband = jnp.concatenate([
                vp_ref[:, hc] if qb == 0 else v_ref[prev_rows, hc],
                v_ref[rows, hc],
                vn_ref[:, hc] if qb == nblk - 1 else v_ref[next_rows, hc]], axis=0)
            qs = jnp.concatenate(
                [q_ref[rows, (GROUP * h + g) * HEAD_DIM:(GROUP * h + g + 1) * HEAD_DIM] for g in range(GROUP)],
                axis=0)
            s = lax.dot_general(qs, kband, (((1,), (1,)), ((), ())), preferred_element_type=F32)
            s = s + bias_ref[h]
            if qb == 0:
                s = s + first_edge
            if qb == nblk - 1:
                s = s + last_edge
            sink = sink_ref[h]
            m = jnp.maximum(jnp.max(s, axis=-1, keepdims=True), sink)
            p = jnp.exp(s - m)
            denom = jnp.sum(p, axis=-1, keepdims=True) + jnp.exp(sink - m)
            o = jnp.dot(p.astype(BF16), vband, preferred_element_type=F32) / denom
            for g in range(GROUP):
                cs = slice((GROUP * h + g) * HEAD_DIM, (GROUP * h + g + 1) * HEAD_DIM)
                act_ref[rows, cs] = (o[g * BLOCK:(g + 1) * BLOCK] * ga_ref[rows, cs].astype(F32)).astype(BF16)
    act_ref[:, ATTN_WIDTH:] = fo_ref[...]
    y_ref[...] = x_ref[...] + jnp.dot(act_ref[...], wout_ref[...], preferred_element_type=F32)


def _even_out(z, fo, x2d, bias_tbl, sink_col, w_out, batch, seq):
    t = x2d.shape[0]
    tm = min(OUT_TM, seq)
    nblk = tm // BLOCK
    nt = seq // tm
    n_blocks = t // BLOCK
    kern = functools.partial(_even_out_kernel, nblk=nblk, nt=nt)

    def row(b, i):
        return b * nt + i

    def prev_blk(b, i):
        return jnp.maximum(row(b, i) * nblk - 1, 0)

    def next_blk(b, i):
        return jnp.minimum((row(b, i) + 1) * nblk, n_blocks - 1)

    kcol, vcol = E_K // KV_WIDTH, E_V // KV_WIDTH
    return pl.pallas_call(
        kern,
        out_shape=jax.ShapeDtypeStruct((t, D_MODEL), F32),
        grid=(batch, nt),
        in_specs=[
            pl.BlockSpec((tm, ATTN_WIDTH), lambda b, i: (row(b, i), E_Q // ATTN_WIDTH)),
            pl.BlockSpec((tm, KV_WIDTH), lambda b, i: (row(b, i), kcol)),
            pl.BlockSpec((BLOCK, KV_WIDTH), lambda b, i: (prev_blk(b, i), kcol)),
            pl.BlockSpec((BLOCK, KV_WIDTH), lambda b, i: (next_blk(b, i), kcol)),
            pl.BlockSpec((tm, KV_WIDTH), lambda b, i: (row(b, i), vcol)),
            pl.BlockSpec((BLOCK, KV_WIDTH), lambda b, i: (prev_blk(b, i), vcol)),
            pl.BlockSpec((BLOCK, KV_WIDTH), lambda b, i: (next_blk(b, i), vcol)),
            pl.BlockSpec((tm, ATTN_WIDTH), lambda b, i: (row(b, i), E_GA // ATTN_WIDTH)),
            pl.BlockSpec((tm, FOURIER_WIDTH), lambda b, i: (row(b, i), 0)),
            pl.BlockSpec((tm, D_MODEL), lambda b, i: (row(b, i), 0)),
            _const_spec(bias_tbl.shape),
            _const_spec(sink_col.shape),
            _const_spec(w_out.shape),
        ],
        out_specs=pl.BlockSpec((tm, D_MODEL), lambda b, i: (row(b, i), 0)),
        scratch_shapes=[pltpu.VMEM((tm, D_MODEL), BF16)],
        compiler_params=pltpu.CompilerParams(
            dimension_semantics=("arbitrary", "arbitrary"), vmem_limit_bytes=V7X_VMEM_LIMIT_BYTES),
        name="even_out",
    )(z, z, z, z, z, z, z, z, fo, x2d, bias_tbl, sink_col, w_out)


HALO = 16


def _odd_out_kernel(z_ref, hp_ref, hn_ref, cp_ref, cn_ref, x_ref, convw_ref, ws_ref, bs_ref, wout_ref,
                    y_ref, act_ref, *, nchunk, nt):
    i = pl.program_id(1)
    tm = nchunk * BLOCK
    row = lax.broadcasted_iota(jnp.int32, (tm, 1), 0)
    has_prev = (i > 0).astype(F32)
    has_next = (i < nt - 1).astype(F32)
    for c0 in range(0, CONV_WIDTH, HEAD_DIM):
        cs = slice(c0, c0 + HEAD_DIM)
        c = z_ref[:, O_CG + c0:O_CG + c0 + HEAD_DIM].astype(F32) * z_ref[:, O_H + c0:O_H + c0 + HEAD_DIM].astype(F32)
        c_before = cp_ref[HALO - 1:HALO, cs].astype(F32) * hp_ref[HALO - 1:HALO, cs].astype(F32) * has_prev
        c_after = cn_ref[0:1, cs].astype(F32) * hn_ref[0:1, cs].astype(F32) * has_next
        c_m1 = jnp.where(row == 0, c_before, pltpu.roll(c, 1, 0))
        c_p1 = jnp.where(row == tm - 1, c_after, pltpu.roll(c, tm - 1, 0))
        conv = c_m1 * convw_ref[0:1, cs] + c * convw_ref[1:2, cs] + c_p1 * convw_ref[2:3, cs]
        co = (z_ref[:, O_BG + c0:O_BG + c0 + HEAD_DIM].astype(F32) * conv) * z_ref[:, O_GC + c0:O_GC + c0 + HEAD_DIM].astype(F32)
        act_ref[:, cs] = co.astype(BF16)
    for ck in range(nchunk):
        rows = slice(ck * BLOCK, (ck + 1) * BLOCK)
        for g in range(N_SG):
            vn = z_ref[rows, O_V + g * SG_DIM:O_V + (g + 1) * SG_DIM]
            sv = jnp.dot(ws_ref[g], vn, preferred_element_type=F32) + bs_ref[g]
            u = z_ref[rows, O_U + g * SG_DIM:O_U + (g + 1) * SG_DIM].astype(F32)
            gd = z_ref[rows, O_GD + g * SG_DIM:O_GD + (g + 1) * SG_DIM].astype(F32)
            act_ref[rows, CONV_WIDTH + g * SG_DIM:CONV_WIDTH + (g + 1) * SG_DIM] = ((u * sv) * gd).astype(BF16)
    y_ref[...] = x_ref[...] + jnp.dot(act_ref[...], wout_ref[...], preferred_element_type=F32)


def _odd_out(z, x2d, conv_w, w_s, b_s, w_out, batch, seq):
    t = x2d.shape[0]
    tm = min(OUT_TM, seq)
    nchunk = tm // BLOCK
    nt = seq // tm
    per = tm // HALO
    n_halo = t // HALO
    kern = functools.partial(_odd_out_kernel, nchunk=nchunk, nt=nt)

    def row(b, i):
        return b * nt + i

    def prev_blk(b, i):
        return jnp.maximum(row(b, i) * per - 1, 0)

    def next_blk(b, i):
        return jnp.minimum((row(b, i) + 1) * per, n_halo - 1)

    hcol, ccol = O_H // CONV_WIDTH, O_CG // CONV_WIDTH
    return pl.pallas_call(
        kern,
        out_shape=jax.ShapeDtypeStruct((t, D_MODEL), F32),
        grid=(batch, nt),
        in_specs=[
            pl.BlockSpec((tm, ODD_IN), lambda b, i: (row(b, i), 0)),
            pl.BlockSpec((HALO, CONV_WIDTH), lambda b, i: (prev_blk(b, i), hcol)),
            pl.BlockSpec((HALO, CONV_WIDTH), lambda b, i: (next_blk(b, i), hcol)),
            pl.BlockSpec((HALO, CONV_WIDTH), lambda b, i: (prev_blk(b, i), ccol)),
            pl.BlockSpec((HALO, CONV_WIDTH), lambda b, i: (next_blk(b, i), ccol)),
            pl.BlockSpec((tm, D_MODEL), lambda b, i: (row(b, i), 0)),
            _const_spec(conv_w.shape),
            _const_spec(w_s.shape),
            _const_spec(b_s.shape),
            _const_spec(w_out.shape),
        ],
        out_specs=pl.BlockSpec((tm, D_MODEL), lambda b, i: (row(b, i), 0)),
        scratch_shapes=[pltpu.VMEM((tm, D_MODEL), BF16)],
        compiler_params=pltpu.CompilerParams(
            dimension_semantics=("arbitrary", "arbitrary"), vmem_limit_bytes=V7X_VMEM_LIMIT_BYTES),
        name="odd_out",
    )(z, z, z, z, z, x2d, conv_w, w_s, b_s, w_out)


def _t5_bucket(rel):
    nb = N_BUCKETS // 2
    ret = (rel > 0).astype(np.int32) * nb
    n = np.abs(rel)
    max_exact = nb // 2
    large = max_exact + (np.log(np.maximum(n, 1) / max_exact) / np.log(MAX_DISTANCE / max_exact)
                         * (nb - max_exact)).astype(np.int32)
    large = np.minimum(large, nb - 1)
    return (ret + np.where(n < max_exact, n, large)).astype(np.int32)


def _bias_table_kernel(relb_ref, bucket_ref, o_ref):
    bucket = bucket_ref[...]
    for head in range(N_HEADS):
        acc = jnp.full((BLOCK, 3 * BLOCK), NEG, F32)
        for b in range(N_BUCKETS):
            acc = jnp.where(bucket == b, relb_ref[b, head], acc)
        h, g = divmod(head, GROUP)
        o_ref[h, g * BLOCK:(g + 1) * BLOCK, :] = acc


def _attention_bias_table(rel_bias):
    r = np.arange(BLOCK)[:, None]
    c = np.arange(3 * BLOCK)[None, :]
    rel = c - BLOCK - r
    bucket = np.where(np.abs(rel) <= BLOCK, _t5_bucket(rel), -1).astype(np.int32)
    return pl.pallas_call(
        _bias_table_kernel,
        out_shape=jax.ShapeDtypeStruct((N_KV_HEADS, GROUP * BLOCK, 3 * BLOCK), F32),
        in_specs=[pl.BlockSpec(memory_space=pltpu.SMEM), pl.BlockSpec(memory_space=pltpu.VMEM)],
        out_specs=pl.BlockSpec(memory_space=pltpu.VMEM),
        name="bias_table",
    )(rel_bias.astype(F32), jnp.asarray(bucket))


def _dft_tables(n, scale):
    fine = 1
    while fine * fine < n:
        fine *= 2
    coarse = n // fine
    k = jnp.arange(n, dtype=jnp.int32)[None, :]
    ja = (jnp.arange(coarse, dtype=jnp.int32) * fine)[:, None]
    jb = jnp.arange(fine, dtype=jnp.int32)[:, None]
    w = 2.0 * math.pi / n
    ang_a = ((ja * k) % n).astype(F32) * w
    ang_b = ((jb * k) % n).astype(F32) * w
    ca, sa = (jnp.cos(ang_a) * scale)[:, None, :], (jnp.sin(ang_a) * scale)[:, None, :]
    cb, sb = jnp.cos(ang_b)[None, :, :], jnp.sin(ang_b)[None, :, :]
    cos = (ca * cb - sa * sb).reshape(n, n)
    sin = (sa * cb + ca * sb).reshape(n, n)
    return cos.astype(BF16), sin.astype(BF16)


def _trunk(x, p, dft_s, batch, seq):
    x2d = x.reshape(batch * seq, D_MODEL)
    for l in range(4):
        i = l // 2
        g = p["norm_gain"][l].reshape(1, D_MODEL)
        if l % 2 == 0:
            z = _in_proj(x2d, g, p["w_in_e"][i], p["qk_gain"][i], 0, (3, 4))
            fo = _fourier(z, dft_s[0], dft_s[1], p["dft_c"][0], p["dft_c"][1], p["w_f"][i], p["b_f"][i], batch, seq)
            x2d = _even_out(z, fo, x2d, p["bias_tbl"], p["sink_col"][i], p["w_out_e"][i], batch, seq)
        else:
            z = _in_proj(x2d, g, p["w_in_o"][i], p["v_gain"][i], O_V // IN_TN, (O_GC // IN_TN, O_GD // IN_TN))
            x2d = _odd_out(z, x2d, p["conv_w"][i], p["w_s"][i], p["b_s"][i], p["w_out_o"][i], batch, seq)
    return x2d.reshape(batch, seq, D_MODEL)


def _prepare_params(norm_gain, rel_bias, w_in_e, w_out_e, q_gain, k_gain, sink, w_f, b_f,
                    w_in_o, conv_w, v_gain, w_s, b_s, w_out_o):
    n_even, n_odd = w_in_e.shape[0], w_in_o.shape[0]
    scale = HEAD_DIM ** -0.5
    qk_gain = jnp.concatenate(
        [jnp.tile(q_gain.astype(F32) * scale, (1, N_HEADS)), jnp.tile(k_gain.astype(F32), (1, N_KV_HEADS))], axis=1)
    return {
        "norm_gain": norm_gain.astype(F32),
        "w_in_e": [w_in_e[i].astype(BF16) for i in range(n_even)],
        "w_out_e": [w_out_e[i].astype(BF16) for i in range(n_even)],
        "w_in_o": [w_in_o[i].astype(BF16) for i in range(n_odd)],
        "w_out_o": [w_out_o[i].astype(BF16) for i in range(n_odd)],
        "qk_gain": qk_gain.reshape(n_even, 1, ATTN_WIDTH + KV_WIDTH),
        "v_gain": v_gain.astype(F32).reshape(n_odd, 1, SG_WIDTH),
        "bias_tbl": _attention_bias_table(rel_bias),
        "sink_col": jnp.repeat(sink.astype(F32).reshape(n_even, N_KV_HEADS, GROUP), BLOCK, axis=2)[..., None],
        "w_f": w_f.astype(BF16),
        "b_f": b_f.astype(F32).reshape(n_even, 1, FOURIER_WIDTH),
        "dft_c": _dft_tables(FOURIER_DIM, FOURIER_DIM ** -0.5),
        "conv_w": conv_w.astype(F32),
        "w_s": w_s.astype(BF16),
        "b_s": jnp.broadcast_to(b_s.astype(F32)[..., None], b_s.shape + (SG_DIM,)),
    }


def _run_trunk(x, p):
    batch, seq = x.shape[0], x.shape[1]
    return _trunk(x, p, _dft_tables(seq, seq ** -0.5), batch, seq)


def kernel(x_prompt, x_sample, norm_gain, rel_bias, w_in_e, w_out_e, q_gain, k_gain, sink, w_f, b_f,
           w_in_o, conv_w, v_gain, w_s, b_s, w_out_o):
    p = _prepare_params(norm_gain, rel_bias, w_in_e, w_out_e, q_gain, k_gain, sink, w_f, b_f,
                        w_in_o, conv_w, v_gain, w_s, b_s, w_out_o)
    return (_run_trunk(x_prompt, p), _run_trunk(x_sample, p))
```
